```python
import math
import jax, jax.numpy as jnp
from jax import lax
import numpy as np

D_MODEL = 1024
BATCH = 32
SEQ = 2048
DEPTH = 1
DEC_BATCH = 8
DEC_SEQ = 32
PAST_LEN = 2048

CHUNK = 64
Q_BLOCK = 128
SB_HEADS = 8
SB_HEAD_DIM = 64
SB_WIDTH = SB_HEADS * SB_HEAD_DIM
SSM_HEADS = 8
SSM_HEAD_DIM = 64
SSM_WIDTH = SSM_HEADS * SSM_HEAD_DIM
SSM_GROUPS = 2
SSM_HEADS_PER_GROUP = SSM_HEADS // SSM_GROUPS
D_STATE = 128
CONV_WIDTH = 4
CONV_DIM = SSM_WIDTH + 2 * SSM_GROUPS * D_STATE
MIX_WIDTH = SB_WIDTH + SSM_WIDTH
D_FF = -(-8 * D_MODEL // (3 * 256)) * 256
N_MOD = 6
SPLITS = [SB_WIDTH, 2 * SB_WIDTH, 3 * SB_WIDTH, 3 * SB_WIDTH + SSM_WIDTH,
          3 * SB_WIDTH + SSM_WIDTH + CONV_DIM]
IN_DIM = 3 * SB_WIDTH + SSM_WIDTH + CONV_DIM + SSM_HEADS
EPS = 1e-6

kernel_name = "hybrid_stickbreak_ssd_adaln_stream_step"


def rms_norm(x, g):
    xf = x.astype(jnp.float32)
    y = xf * lax.rsqrt(jnp.mean(xf * xf, axis=-1, keepdims=True) + EPS)
    return (y * g.astype(jnp.float32)).astype(x.dtype)


def stick_breaking(q, k, v, q_start):
    tq, tk = q.shape[1], k.shape[1]
    z = jnp.einsum("bqhd,bkhd->bhqk", q, k).astype(jnp.float32) * (SB_HEAD_DIM ** -0.5)
    q_pos = q_start + jnp.arange(tq)
    k_pos = jnp.arange(tk)
    mask = k_pos[None, :] < q_pos[:, None]
    log_beta = jax.nn.log_sigmoid(z)
    log_1mb = jnp.where(mask, log_beta - z, 0.0)
    after = lax.cumsum(log_1mb, axis=3, reverse=True) - log_1mb
    w = jnp.where(mask, jnp.exp(log_beta + after), 0.0)
    return jnp.einsum("bhqk,bkhd->bqhd", w.astype(v.dtype), v)


def ssd_scan(x, dt, a, b_in, c_in, init_state, chunk):
    bsz, l = x.shape[0], x.shape[1]
    nc = l // chunk
    xc = x.reshape(bsz, nc, chunk, SSM_GROUPS, SSM_HEADS_PER_GROUP, SSM_HEAD_DIM)
    dtc = dt.reshape(bsz, nc, chunk, SSM_GROUPS, SSM_HEADS_PER_GROUP)
    bc = b_in.reshape(bsz, nc, chunk, SSM_GROUPS, D_STATE)
    cc = c_in.reshape(bsz, nc, chunk, SSM_GROUPS, D_STATE)
    acum = jnp.cumsum(dtc * a, axis=2)
    causal = jnp.tril(jnp.ones((chunk, chunk), dtype=bool))[:, :, None, None]
    seg = acum[:, :, :, None] - acum[:, :, None, :]
    decay = jnp.exp(jnp.where(causal, seg, -jnp.inf))
    cb = jnp.einsum("bctgn,bcsgn->bctsg", cc, bc).astype(jnp.float32)
    mix = cb[..., None] * decay * dtc[:, :, None]
    y_diag = jnp.einsum("bctsgj,bcsgjp->bctgjp", mix, xc)
    xw = xc * (jnp.exp(acum[:, :, -1:] - acum) * dtc)[..., None]
    chunk_states = jnp.einsum("bctgn,bctgjp->bcgjpn", bc, xw)
    chunk_decay = jnp.exp(acum[:, :, -1])

    def step(state, inp):
        st, dec = inp
        return state * dec[..., None, None] + st, state

    final, prev = lax.scan(step, init_state,
                           (jnp.moveaxis(chunk_states, 1, 0), jnp.moveaxis(chunk_decay, 1, 0)))
    prev = jnp.moveaxis(prev, 0, 1)
    y_off = jnp.einsum("bctgn,bcgjpn->bctgjp", cc, prev) * jnp.exp(acum)[..., None]
    y = (y_diag + y_off).reshape(bsz, l, SSM_GROUPS, SSM_HEADS_PER_GROUP, SSM_HEAD_DIM)
    return y, final


def ssd_mixer(z, xbc_raw, dt_raw, conv_past, ssm_past, conv_w, conv_b, dt_bias, a_log, d_skip, g_ssm_out):
    bsz, l = xbc_raw.shape[0], xbc_raw.shape[1]
    full = jnp.concatenate([conv_past.astype(xbc_raw.dtype), xbc_raw], axis=1)
    conv = conv_b
    for i in range(CONV_WIDTH):
        conv = conv + full[:, i:i + l] * conv_w[i]
    new_conv = full[:, l:]
    xbc = jax.nn.silu(conv)
    xs, bs, cs = jnp.split(xbc, [SSM_WIDTH, SSM_WIDTH + SSM_GROUPS * D_STATE], axis=-1)
    x = xs.reshape(bsz, l, SSM_GROUPS, SSM_HEADS_PER_GROUP, SSM_HEAD_DIM)
    b_in = bs.reshape(bsz, l, SSM_GROUPS, D_STATE)
    c_in = cs.reshape(bsz, l, SSM_GROUPS, D_STATE)
    dt = jax.nn.softplus(dt_raw.astype(jnp.float32) + dt_bias.astype(jnp.float32))
    dt = dt.reshape(bsz, l, SSM_GROUPS, SSM_HEADS_PER_GROUP)
    a = -jnp.exp(a_log.astype(jnp.float32)).reshape(SSM_GROUPS, SSM_HEADS_PER_GROUP)
    init = ssm_past.astype(jnp.float32).reshape(bsz, SSM_GROUPS, SSM_HEADS_PER_GROUP, SSM_HEAD_DIM, D_STATE)
    chunk = CHUNK if l % CHUNK == 0 else l
    y, final = ssd_scan(x, dt, a, b_in, c_in, init, chunk)
    y = y + d_skip.reshape(SSM_GROUPS, SSM_HEADS_PER_GROUP)[..., None] * x
    y = rms_norm(y.reshape(bsz, l, SSM_WIDTH) * jax.nn.silu(z), g_ssm_out)
    return y, new_conv, final.reshape(bsz, SSM_HEADS, SSM_HEAD_DIM, D_STATE)


def trunk_layer(x, c, k_past, v_past, conv_past, ssm_past,
                w_ada, b_ada, g_norm1, w_in, g_q, g_k, g_attn_out, conv_w, conv_b,
                dt_bias, a_log, d_skip, g_ssm_out, w_out, g_norm2, w_gate, w_up, w_down):
    bsz, l, _ = x.shape
    mod = (jax.nn.silu(c) @ w_ada + b_ada)[:, None, :]
    shift1, scale1, gate1, shift2, scale2, gate2 = jnp.split(mod, N_MOD, axis=-1)
    h = rms_norm(x, g_norm1) * (1 + scale1) + shift1
    q, k, v, z, xbc_raw, dt_raw = jnp.split(h @ w_in, SPLITS, axis=-1)
    q = rms_norm(q.reshape(bsz, l, SB_HEADS, SB_HEAD_DIM), g_q)
    k = rms_norm(k.reshape(bsz, l, SB_HEADS, SB_HEAD_DIM), g_k)
    v = v.reshape(bsz, l, SB_HEADS, SB_HEAD_DIM)
    if k_past is None:
        attn = jnp.concatenate(
            [stick_breaking(q[:, s:s + Q_BLOCK], k[:, :s + Q_BLOCK], v[:, :s + Q_BLOCK], s)
             for s in range(0, l, Q_BLOCK)], axis=1)
        conv_past = jnp.zeros((bsz, CONV_WIDTH - 1, CONV_DIM), xbc_raw.dtype)
        ssm_past = jnp.zeros((bsz, SSM_HEADS, SSM_HEAD_DIM, D_STATE), jnp.float32)
    else:
        k_all = jnp.concatenate([k_past, k], axis=1)
        v_all = jnp.concatenate([v_past, v], axis=1)
        attn = stick_breaking(q, k_all, v_all, k_past.shape[1])
    attn = rms_norm(attn.reshape(bsz, l, SB_WIDTH), g_attn_out)
    ssm, new_conv, new_ssm = ssd_mixer(z, xbc_raw, dt_raw, conv_past, ssm_past, conv_w, conv_b,
                                       dt_bias, a_log, d_skip, g_ssm_out)
    mixed = jnp.concatenate([attn.astype(ssm.dtype), ssm], axis=-1) @ w_out
    x = x + gate1 * mixed
    h2 = rms_norm(x, g_norm2) * (1 + scale2) + shift2
    ff = (jax.nn.silu(h2 @ w_gate) * (h2 @ w_up)) @ w_down
    x = x + gate2 * ff
    return x, k, v, new_conv, new_ssm


def setup_inputs(seed: int = 0) -> dict:
    key = jax.random.key(seed)
    ks = jax.random.split(key, 32)
    f32 = jnp.float32

    def nrm(k, shape, scale):
        return jax.random.normal(k, shape, f32) * scale

    dt0 = jnp.exp(jax.random.uniform(ks[16], (DEPTH, SSM_HEADS), f32, math.log(1e-3), math.log(1e-1)))
    return {
        "x_prompt": nrm(ks[0], (BATCH, SEQ, D_MODEL), 1.0),
        "x_sample": nrm(ks[1], (DEC_BATCH, DEC_SEQ, D_MODEL), 1.0),
        "cache_k": nrm(ks[2], (DEPTH, DEC_BATCH, PAST_LEN, SB_HEADS, SB_HEAD_DIM), 1.0),
        "cache_v": nrm(ks[3], (DEPTH, DEC_BATCH, PAST_LEN, SB_HEADS, SB_HEAD_DIM), 1.0),
        "state_conv": nrm(ks[4], (DEPTH, DEC_BATCH, CONV_WIDTH - 1, CONV_DIM), 1.0),
        "state_ssm": nrm(ks[5], (DEPTH, DEC_BATCH, SSM_HEADS, SSM_HEAD_DIM, D_STATE), 0.5),
        "c_prompt": nrm(ks[6], (BATCH, D_MODEL), 1.0),
        "c_sample": nrm(ks[7], (DEC_BATCH, D_MODEL), 1.0),
        "w_ada": nrm(ks[8], (DEPTH, D_MODEL, N_MOD * D_MODEL), D_MODEL ** -0.5),
        "b_ada": nrm(ks[9], (DEPTH, N_MOD * D_MODEL), 0.01),
        "g_norm1": 1.0 + nrm(ks[10], (DEPTH, D_MODEL), 0.02),
        "w_in": nrm(ks[11], (DEPTH, D_MODEL, IN_DIM), D_MODEL ** -0.5),
        "g_q": 1.0 + nrm(ks[12], (DEPTH, SB_HEAD_DIM), 0.02),
        "g_k": 1.0 + nrm(ks[13], (DEPTH, SB_HEAD_DIM), 0.02),
        "g_attn_out": 1.0 + nrm(ks[14], (DEPTH, SB_WIDTH), 0.02),
        "conv_w": nrm(ks[15], (DEPTH, CONV_WIDTH, CONV_DIM), CONV_WIDTH ** -0.5),
        "conv_b": nrm(ks[17], (DEPTH, CONV_DIM), 0.01),
        "dt_bias": dt0 + jnp.log(-jnp.expm1(-dt0)),
        "a_log": jnp.log(jax.random.uniform(ks[18], (DEPTH, SSM_HEADS), f32, 1.0, 16.0)),
        "d_skip": 1.0 + nrm(ks[19], (DEPTH, SSM_HEADS), 0.02),
        "g_ssm_out": 1.0 + nrm(ks[20], (DEPTH, SSM_WIDTH), 0.02),
        "w_out": nrm(ks[21], (DEPTH, MIX_WIDTH, D_MODEL), MIX_WIDTH ** -0.5),
        "g_norm2": 1.0 + nrm(ks[22], (DEPTH, D_MODEL), 0.02),
        "w_gate": nrm(ks[23], (DEPTH, D_MODEL, D_FF), D_MODEL ** -0.5),
        "w_up": nrm(ks[24], (DEPTH, D_MODEL, D_FF), D_MODEL ** -0.5),
        "w_down": nrm(ks[25], (DEPTH, D_FF, D_MODEL), D_FF ** -0.5),
    }


def reference(x_prompt, x_sample, cache_k, cache_v, state_conv, state_ssm, c_prompt, c_sample,
              w_ada, b_ada, g_norm1, w_in, g_q, g_k, g_attn_out, conv_w, conv_b,
              dt_bias, a_log, d_skip, g_ssm_out, w_out, g_norm2, w_gate, w_up, w_down):
    xp, xs = x_prompt, x_sample
    kp_l, vp_l, cp_l, sp_l, ks_l, vs_l, cs_l, ss_l = [], [], [], [], [], [], [], []
    for i in range(DEPTH):
        lw = (w_ada[i], b_ada[i], g_norm1[i], w_in[i], g_q[i], g_k[i], g_attn_out[i], conv_w[i], conv_b[i],
              dt_bias[i], a_log[i], d_skip[i], g_ssm_out[i], w_out[i], g_norm2[i], w_gate[i], w_up[i], w_down[i])
        xp, kp, vp, cp, sp = trunk_layer(xp, c_prompt, None, None, None, None, *lw)
        xs, ks_, vs_, cs_, ss_ = trunk_layer(xs, c_sample, cache_k[i], cache_v[i], state_conv[i], state_ssm[i], *lw)
        kp_l.append(kp); vp_l.append(vp); cp_l.append(cp); sp_l.append(sp)
        ks_l.append(ks_); vs_l.append(vs_); cs_l.append(cs_); ss_l.append(ss_)
    y_prompt = xp.astype(x_prompt.dtype)
    y_sample = xs.astype(x_sample.dtype)
    return (y_prompt, y_sample,
            jnp.stack(kp_l), jnp.stack(vp_l), jnp.stack(cp_l), jnp.stack(sp_l),
            jnp.stack(ks_l), jnp.stack(vs_l), jnp.stack(cs_l), jnp.stack(ss_l))
```

```python
import functools

import jax
import jax.numpy as jnp
from jax import lax
from jax.experimental import pallas as pl
from jax.experimental.pallas import tpu as pltpu

F32 = jnp.float32
BF16 = jnp.bfloat16

LANES = 128
SUBLANES = 8
VMEM_LIMIT_BYTES = 56 * 1024 * 1024

SB_HEADS = 8
SB_HEAD_DIM = 64
SB_WIDTH = SB_HEADS * SB_HEAD_DIM
SSM_HEADS = 8
SSM_HEAD_DIM = 64
SSM_WIDTH = SSM_HEADS * SSM_HEAD_DIM
SSM_GROUPS = 2
D_STATE = 128
CONV_WIDTH = 4
CONV_DIM = SSM_WIDTH + 2 * SSM_GROUPS * D_STATE
N_MOD = 6
EPS = 1e-6

KEY_BLOCK = 128
SSD_CHUNK = 128
HEADS_PER_SLAB = LANES // SB_HEAD_DIM


def _params(*sem):
    return pltpu.CompilerParams(dimension_semantics=sem, vmem_limit_bytes=VMEM_LIMIT_BYTES)


def _const_spec(shape):
    nd = len(shape)
    return pl.BlockSpec(shape, lambda *_: (0,) * nd, pipeline_mode=pl.Buffered(1))


def _split_bf16(v, terms):
    parts = []
    rem = v
    for _ in range(terms):
        p = rem.astype(BF16)
        parts.append(p)
        rem = rem - p.astype(F32)
    return parts


def _softplus(x):
    return jnp.maximum(x, 0.0) + jnp.log1p(jnp.exp(-jnp.abs(x)))


def _silu(x):
    return x / (1.0 + jnp.exp(-x))


def _mod_kernel(c_ref, w_ref, b_ref, o_ref):
    c = c_ref[...]
    a = _silu(c).astype(BF16)
    o_ref[...] = jnp.dot(a, w_ref[...].astype(BF16), preferred_element_type=F32) + b_ref[...]


def _modulation(c, w_ada, b_ada):
    n, d = c.shape
    cols = w_ada.shape[1]
    bn = d
    return pl.pallas_call(
        _mod_kernel,
        grid=(cols // bn,),
        in_specs=[pl.BlockSpec((n, d), lambda j: (0, 0)),
                  pl.BlockSpec((d, bn), lambda j: (0, j)),
                  pl.BlockSpec((1, bn), lambda j: (0, j))],
        out_specs=pl.BlockSpec((n, bn), lambda j: (0, j)),
        out_shape=jax.ShapeDtypeStruct((n, cols), F32),
        compiler_params=_params("parallel"),
        name="adaln_mod",
    )(c, w_ada, b_ada.reshape(1, cols))


def _inproj_kernel(x_ref, mod_ref, g1_ref, w_ref, wdt_hi_ref, wdt_lo_ref, gq_ref, gk_ref, bd_ref,
                   q_ref, k_ref, v_ref, kb_ref, vb_ref, z_ref, xbc_ref, dt_ref):
    x = x_ref[0]
    mod = mod_ref[0]
    shift1 = mod[0:1]
    scale1 = mod[1:2]
    ms = jnp.mean(x * x, axis=-1, keepdims=True)
    h = x * lax.rsqrt(ms + EPS) * g1_ref[...]
    h = h * (1.0 + scale1) + shift1
    h_hi, h_lo = _split_bf16(h, 2)

    def proj(lo, hi):
        return jnp.dot(h_hi, w_ref[:, lo:hi], preferred_element_type=F32)

    def head_norm(t, g):
        msq = jnp.dot((t * t).astype(BF16), bd_ref[...], preferred_element_type=F32)
        return t * lax.rsqrt(msq + EPS) * g

    w = SB_WIDTH
    q_ref[0] = head_norm(proj(0, w), gq_ref[...]).astype(BF16)
    k = head_norm(proj(w, 2 * w), gk_ref[...])
    k_ref[0] = k
    kb_ref[0] = k.astype(BF16)
    v = proj(2 * w, 3 * w)
    v_ref[0] = v
    vb_ref[0] = v.astype(BF16)
    z_ref[0] = proj(3 * w, 3 * w + SSM_WIDTH).astype(BF16)
    xbc_ref[0] = proj(3 * w + SSM_WIDTH, 3 * w + SSM_WIDTH + CONV_DIM)
    dt_ref[0] = (jnp.dot(h_hi, wdt_hi_ref[...], preferred_element_type=F32)
                 + jnp.dot(h_lo, wdt_hi_ref[...], preferred_element_type=F32)
                 + jnp.dot(h_hi, wdt_lo_ref[...], preferred_element_type=F32))


def _in_projection(x, mod, g_norm1, w_main, wdt_hi, wdt_lo, gq, gk, bd, tm):
    b, l, d = x.shape
    main = w_main.shape[1]
    tok = lambda width: pl.BlockSpec((1, tm, width), lambda i, j: (i, j, 0))
    out_shapes = [
        jax.ShapeDtypeStruct((b, l, SB_WIDTH), BF16),
        jax.ShapeDtypeStruct((b, l, SB_WIDTH), F32),
        jax.ShapeDtypeStruct((b, l, SB_WIDTH), F32),
        jax.ShapeDtypeStruct((b, l, SB_WIDTH), BF16),
        jax.ShapeDtypeStruct((b, l, SB_WIDTH), BF16),
        jax.ShapeDtypeStruct((b, l, SSM_WIDTH), BF16),
        jax.ShapeDtypeStruct((b, l, CONV_DIM), F32),
        jax.ShapeDtypeStruct((b, l, LANES), F32),
    ]
    return pl.pallas_call(
        _inproj_kernel,
        grid=(b, l // tm),
        in_specs=[tok(d),
                  pl.BlockSpec((1, N_MOD, d), lambda i, j: (i, 0, 0)),
                  _const_spec((1, d)),
                  _const_spec((d, main)),
                  _const_spec((d, LANES)),
                  _const_spec((d, LANES)),
                  _const_spec((1, SB_WIDTH)),
                  _const_spec((1, SB_WIDTH)),
                  _const_spec((SB_WIDTH, SB_WIDTH))],
        out_specs=[tok(s.shape[-1]) for s in out_shapes],
        out_shape=out_shapes,
        compiler_params=_params("parallel", "parallel"),
        name="in_projection",
    )(x, mod, g_norm1, w_main, wdt_hi, wdt_lo, gq, gk, bd)


def _attn_kernel(q_ref, k_ref, v_ref, tri_ref, g_ref, o_ref, acc_ref, *, tq, q_base):
    qi = pl.program_id(1)
    q0 = q_base + qi * tq
    kb_diag = q0 // KEY_BLOCK
    lane = lax.broadcasted_iota(jnp.int32, (1, LANES), 1)
    head_lanes = [(lane // SB_HEAD_DIM) == e for e in range(HEADS_PER_SLAB)]
    q_pos = q0 + lax.broadcasted_iota(jnp.int32, (tq, KEY_BLOCK), 0)
    k_col = lax.broadcasted_iota(jnp.int32, (tq, KEY_BLOCK), 1)
    tri = tri_ref[...]

    for slab in range(SB_WIDTH // LANES):
        cols = slice(slab * LANES, (slab + 1) * LANES)
        q_slab = q_ref[0, :, cols]
        q_heads = [jnp.where(m, q_slab, jnp.zeros_like(q_slab)) for m in head_lanes]

        def block(kb, carry, masked):
            acc, sums = carry
            start = pl.multiple_of(kb * KEY_BLOCK, KEY_BLOCK)
            k_blk = k_ref[0, pl.ds(start, KEY_BLOCK), cols]
            v_blk = v_ref[0, pl.ds(start, KEY_BLOCK), cols]
            new_sums = []
            for e in range(HEADS_PER_SLAB):
                zs = lax.dot_general(q_heads[e], k_blk, (((1,), (1,)), ((), ())),
                                     preferred_element_type=F32)
                sp = jnp.maximum(zs, 0.0) + jnp.log(1.0 + jnp.exp(-jnp.abs(zs)))
                if masked:
                    visible = (start + k_col) < q_pos
                    sp = jnp.where(visible, sp, 0.0)
                sp_hi, sp_lo = _split_bf16(sp, 2)
                cs = jnp.dot(jnp.concatenate([sp_hi, sp_lo], axis=1), tri, preferred_element_type=F32)
                w = jnp.exp(zs - sp - sums[e] - cs[:, :KEY_BLOCK])
                if masked:
                    w = jnp.where(visible, w, 0.0)
                new_sums.append(sums[e] + cs[:, KEY_BLOCK:])
                v_head = jnp.where(head_lanes[e], v_blk, jnp.zeros_like(v_blk))
                acc = acc + jnp.dot(w.astype(BF16), v_head, preferred_element_type=F32)
            return acc, tuple(new_sums)

        zero = jnp.zeros((tq, LANES), F32)
        carry = block(kb_diag, (zero, (zero,) * HEADS_PER_SLAB), True)
        carry = lax.fori_loop(0, kb_diag, lambda i, c: block(kb_diag - 1 - i, c, False), carry)
        acc_ref[:, cols] = carry[0]

    o = acc_ref[...]
    ms = jnp.mean(o * o, axis=-1, keepdims=True)
    o_ref[0] = (o * lax.rsqrt(ms + EPS) * g_ref[...]).astype(o_ref.dtype)


def _attention(q, k, v, tri, g_attn, tq, q_base):
    b, lq, w = q.shape
    lk = k.shape[1]
    return pl.pallas_call(
        functools.partial(_attn_kernel, tq=tq, q_base=q_base),
        grid=(b, lq // tq),
        in_specs=[pl.BlockSpec((1, tq, w), lambda i, j: (i, j, 0)),
                  pl.BlockSpec((1, lk, w), lambda i, j: (i, 0, 0)),
                  pl.BlockSpec((1, lk, w), lambda i, j: (i, 0, 0)),
                  _const_spec(tri.shape),
                  _const_spec((1, w))],
        out_specs=pl.BlockSpec((1, tq, w), lambda i, j: (i, j, 0)),
        out_shape=jax.ShapeDtypeStruct((b, lq, w), BF16),
        scratch_shapes=[pltpu.VMEM((tq, w), F32)],
        compiler_params=_params("parallel", "parallel"),
        name="stick_breaking_attention",
    )(q, k, v, tri, g_attn)


def _ssd_kernel(xbc_ref, z_ref, dt_ref, convp_ref, ssmp_ref, cw_ref, cb_ref, dtb_ref, alog_ref,
                dsk_ref, g_ref, e_ref, tri_ref,
                y_ref, convo_ref, ssmo_ref, xpad_ref, st_ref, *, t, valid):
    c = pl.program_id(1)
    last = pl.num_programs(1) - 1
    tail = CONV_WIDTH - 1

    @pl.when(c == 0)
    def _():
        xpad_ref[0:SUBLANES, :] = jnp.zeros((SUBLANES, CONV_DIM), F32)
        xpad_ref[SUBLANES - tail:SUBLANES, :] = convp_ref[0]
        st_ref[...] = ssmp_ref[0].reshape(SSM_WIDTH, D_STATE).T

    xr = xbc_ref[0]
    xpad_ref[SUBLANES:SUBLANES + t, :] = xr
    conv = cb_ref[...] + cw_ref[CONV_WIDTH - 1:CONV_WIDTH, :] * xr
    for i in range(tail):
        conv = conv + cw_ref[i:i + 1, :] * xpad_ref[SUBLANES - tail + i:SUBLANES - tail + i + t, :]
    xpad_ref[0:SUBLANES, :] = xr[t - SUBLANES:t, :]
    convo_ref[0] = xr[valid - tail:valid, :]
    xbc = _silu(conv)
    xs = xbc[:, :SSM_WIDTH]
    bm = xbc[:, SSM_WIDTH:SSM_WIDTH + SSM_GROUPS * D_STATE]
    cm = xbc[:, SSM_WIDTH + SSM_GROUPS * D_STATE:]

    lane = lax.broadcasted_iota(jnp.int32, (1, LANES), 1)
    dt = _softplus(dt_ref[0] + dtb_ref[...])
    if valid < t:
        row = lax.broadcasted_iota(jnp.int32, (t, LANES), 0)
        dt = jnp.where(row < valid, dt, 0.0)
    a = jnp.where(lane < SSM_HEADS, -jnp.exp(alog_ref[...]), 0.0)
    tri = tri_ref[...]
    acum = sum(jnp.dot(tri, p, preferred_element_type=F32) for p in _split_bf16(dt * a, 3))
    acum_last = acum[t - 1:t, :]
    acum_t = acum.T
    dt_t = dt.T

    def expand(v):
        return sum(jnp.dot(p, e_ref[...], preferred_element_type=F32) for p in _split_bf16(v, 2))

    decay_in = expand(jnp.exp(acum))
    weight = expand(jnp.exp(acum_last - acum) * dt)
    chunk_decay = decay_in[t - 1:t, :]

    xs_b = xs.astype(BF16)
    xw = (xs * weight).astype(BF16)
    st = st_ref[...]
    st_b = st.astype(BF16)
    causal = (lax.broadcasted_iota(jnp.int32, (t, t), 0) >= lax.broadcasted_iota(jnp.int32, (t, t), 1))
    head_lanes = [(lane // SSM_HEAD_DIM) == e for e in range(HEADS_PER_SLAB)]
    group_w = SSM_WIDTH // SSM_GROUPS
    heads_per_group = SSM_HEADS // SSM_GROUPS

    y_diag = []
    y_off = []
    new_state = []
    for g in range(SSM_GROUPS):
        b_g = bm[:, g * D_STATE:(g + 1) * D_STATE]
        c_g = cm[:, g * D_STATE:(g + 1) * D_STATE].astype(BF16)
        b_gt = b_g.T.astype(BF16)
        gcols = slice(g * group_w, (g + 1) * group_w)
        new_state.append(jnp.dot(b_gt, xw[:, gcols], preferred_element_type=F32))
        y_off.append(jnp.dot(c_g, st_b[:, gcols], preferred_element_type=F32))
        cb = jnp.dot(c_g, b_gt, preferred_element_type=F32)
        for slab in range(heads_per_group // HEADS_PER_SLAB):
            mixes = []
            for e in range(HEADS_PER_SLAB):
                hd = g * heads_per_group + slab * HEADS_PER_SLAB + e
                seg = acum[:, hd:hd + 1] - acum_t[hd:hd + 1, :]
                decay = jnp.exp(jnp.where(causal, seg, -jnp.inf))
                mixes.append((cb * decay * dt_t[hd:hd + 1, :]).astype(BF16))
            s0 = (g * heads_per_group // HEADS_PER_SLAB + slab) * LANES
            x_slab = xs_b[:, s0:s0 + LANES]
            x_heads = [jnp.where(m, x_slab, jnp.zeros_like(x_slab)) for m in head_lanes]
            y_diag.append(jnp.dot(jnp.concatenate(mixes, axis=1), jnp.concatenate(x_heads, axis=0),
                                  preferred_element_type=F32))

    st_new = st * chunk_decay + jnp.concatenate(new_state, axis=1)
    st_ref[...] = st_new

    @pl.when(c == last)
    def _():
        ssmo_ref[0] = st_new.T.reshape(SSM_HEADS, SSM_HEAD_DIM, D_STATE)

    y = (jnp.concatenate(y_diag, axis=1) + jnp.concatenate(y_off, axis=1) * decay_in
         + dsk_ref[...] * xs)
    yg = y * _silu(z_ref[0].astype(F32))
    ms = jnp.mean(yg * yg, axis=-1, keepdims=True)
    y_ref[0] = (yg * lax.rsqrt(ms + EPS) * g_ref[...]).astype(y_ref.dtype)


def _ssd_mixer(xbc, z, dt, conv_past, ssm_past, conv_w, conv_b, dtb, alog, dsk, g_ssm, expand_mat, tri, valid):
    b, l, _ = xbc.shape
    t = SSD_CHUNK
    tail = CONV_WIDTH - 1
    assert valid >= tail and (valid == l or l == t)
    tok = lambda width: pl.BlockSpec((1, t, width), lambda i, j: (i, j, 0))
    out_shapes = [
        jax.ShapeDtypeStruct((b, l, SSM_WIDTH), BF16),
        jax.ShapeDtypeStruct((b, tail, CONV_DIM), F32),
        jax.ShapeDtypeStruct((b, SSM_HEADS, SSM_HEAD_DIM, D_STATE), F32),
    ]
    return pl.pallas_call(
        functools.partial(_ssd_kernel, t=t, valid=min(valid, t)),
        grid=(b, l // t),
        in_specs=[tok(CONV_DIM), tok(SSM_WIDTH), tok(LANES),
                  pl.BlockSpec((1, tail, CONV_DIM), lambda i, j: (i, 0, 0)),
                  pl.BlockSpec((1, SSM_HEADS, SSM_HEAD_DIM, D_STATE), lambda i, j: (i, 0, 0, 0)),
                  _const_spec((CONV_WIDTH, CONV_DIM)),
                  _const_spec((1, CONV_DIM)),
                  _const_spec((1, LANES)),
                  _const_spec((1, LANES)),
                  _const_spec((1, SSM_WIDTH)),
                  _const_spec((1, SSM_WIDTH)),
                  _const_spec(expand_mat.shape),
                  _const_spec(tri.shape)],
        out_specs=[tok(SSM_WIDTH),
                   pl.BlockSpec((1, tail, CONV_DIM), lambda i, j: (i, 0, 0)),
                   pl.BlockSpec((1, SSM_HEADS, SSM_HEAD_DIM, D_STATE), lambda i, j: (i, 0, 0, 0))],
        out_shape=out_shapes,
        scratch_shapes=[pltpu.VMEM((SUBLANES + t, CONV_DIM), F32),
                        pltpu.VMEM((D_STATE, SSM_WIDTH), F32)],
        compiler_params=_params("parallel", "arbitrary"),
        name="ssd_mixer",
    )(xbc, z, dt, conv_past, ssm_past, conv_w, conv_b, dtb, alog, dsk, g_ssm, expand_mat, tri)


def _ffn_chunks(d_ff):
    step = 4 * 256
    return [(s, min(s + step, d_ff)) for s in range(0, d_ff, step)]


def _outffn_kernel(x_ref, attn_ref, ssm_ref, mod_ref, g2_ref, wo_ref, wg_ref, wu_ref, wd_ref, y_ref):
    x = x_ref[0]
    mod = mod_ref[0]
    gate1, shift2, scale2, gate2 = mod[2:3], mod[3:4], mod[4:5], mod[5:6]
    mixed = (jnp.dot(attn_ref[0], wo_ref[:SB_WIDTH, :], preferred_element_type=F32)
             + jnp.dot(ssm_ref[0], wo_ref[SB_WIDTH:, :], preferred_element_type=F32))
    x1 = x + gate1 * mixed
    ms = jnp.mean(x1 * x1, axis=-1, keepdims=True)
    h2 = x1 * lax.rsqrt(ms + EPS) * g2_ref[...]
    h2 = (h2 * (1.0 + scale2) + shift2).astype(BF16)
    ff = jnp.zeros_like(x)
    for lo, hi in _ffn_chunks(wg_ref.shape[1]):
        gate = jnp.dot(h2, wg_ref[:, lo:hi], preferred_element_type=F32)
        up = jnp.dot(h2, wu_ref[:, lo:hi], preferred_element_type=F32)
        act = (_silu(gate) * up).astype(BF16)
        ff = ff + jnp.dot(act, wd_ref[lo:hi, :], preferred_element_type=F32)
    y_ref[0] = x1 + gate2 * ff


def _out_ffn(x, attn, ssm, mod, g_norm2, w_out, w_gate, w_up, w_down, tm):
    b, l, d = x.shape
    d_ff = w_gate.shape[1]
    tok = lambda width: pl.BlockSpec((1, tm, width), lambda i, j: (i, j, 0))
    return pl.pallas_call(
        _outffn_kernel,
        grid=(b, l // tm),
        in_specs=[tok(d), tok(SB_WIDTH), tok(SSM_WIDTH),
                  pl.BlockSpec((1, N_MOD, d), lambda i, j: (i, 0, 0)),
                  _const_spec((1, d)),
                  _const_spec(w_out.shape),
                  _const_spec((d, d_ff)),
                  _const_spec((d, d_ff)),
                  _const_spec((d_ff, d))],
        out_specs=tok(d),
        out_shape=jax.ShapeDtypeStruct((b, l, d), F32),
        compiler_params=_params("parallel", "parallel"),
        name="out_proj_ffn",
    )(x, attn, ssm, mod, g_norm2, w_out, w_gate, w_up, w_down)


def _token_tile(l):
    return min(l, 512)


def _pad_rows(a, rows):
    return jnp.pad(a, ((0, 0), (0, rows - a.shape[1]), (0, 0)))


def _layer(x, mod, k_past, v_past, conv_past, ssm_past, wts, consts):
    b, l, d = x.shape
    tm = _token_tile(l)
    q, k, v, kb, vb, z, xbc, dt = _in_projection(
        x, mod, wts["g_norm1"], wts["w_main"], wts["wdt_hi"], wts["wdt_lo"],
        wts["gq"], wts["gk"], consts["head_mean"], tm)

    if k_past is None:
        attn = _attention(q, kb, vb, consts["attn_tri"], wts["g_attn_out"], KEY_BLOCK, 0)
    else:
        past = k_past.shape[1]
        assert past % KEY_BLOCK == 0 and l <= KEY_BLOCK
        k_all = jnp.concatenate([k_past.astype(BF16), _pad_rows(kb, KEY_BLOCK)], axis=1)
        v_all = jnp.concatenate([v_past.astype(BF16), _pad_rows(vb, KEY_BLOCK)], axis=1)
        attn = _attention(q, k_all, v_all, consts["attn_tri"], wts["g_attn_out"], l, past)

    if l % SSD_CHUNK:
        assert l < SSD_CHUNK
        xbc_p, z_p, dt_p = (_pad_rows(t, SSD_CHUNK) for t in (xbc, z, dt))
    else:
        xbc_p, z_p, dt_p = xbc, z, dt
    ssm, new_conv, new_ssm = _ssd_mixer(
        xbc_p, z_p, dt_p, conv_past, ssm_past, wts["conv_w"], wts["conv_b"], wts["dt_bias"],
        wts["a_log"], wts["d_skip"], wts["g_ssm_out"], consts["expand"], consts["scan_tri"], l)
    ssm = ssm[:, :l]

    y = _out_ffn(x, attn, ssm, mod, wts["g_norm2"], wts["w_out"], wts["w_gate"], wts["w_up"],
                 wts["w_down"], tm)
    hd = (b, l, SB_HEADS, SB_HEAD_DIM)
    return y, k.reshape(hd), v.reshape(hd), new_conv, new_ssm


def _constants():
    i = jnp.arange(2 * KEY_BLOCK)
    m, j = (i % KEY_BLOCK)[:, None], i[None, :]
    attn_tri = ((j >= KEY_BLOCK) | (m > j)).astype(BF16)
    r = jnp.arange(SSD_CHUNK)
    scan_tri = (r[None, :] <= r[:, None]).astype(BF16)
    heads = jnp.arange(LANES)[:, None]
    expand = (heads == (jnp.arange(SSM_WIDTH)[None, :] // SSM_HEAD_DIM)).astype(BF16)
    c = jnp.arange(SB_WIDTH) // SB_HEAD_DIM
    head_mean = ((c[:, None] == c[None, :]).astype(F32) / SB_HEAD_DIM).astype(BF16)
    return dict(attn_tri=attn_tri, scan_tri=scan_tri, expand=expand, head_mean=head_mean)


def _pad_lanes(v):
    return jnp.pad(v, (0, LANES - v.shape[0])).reshape(1, LANES)


def _layer_weights(w_in, g_norm1, g_q, g_k, g_attn_out, conv_w, conv_b, dt_bias, a_log, d_skip,
                   g_ssm_out, w_out, g_norm2, w_gate, w_up, w_down):
    d = w_in.shape[0]
    main = 3 * SB_WIDTH + SSM_WIDTH + CONV_DIM
    w_dt = jnp.pad(w_in[:, main:], ((0, 0), (0, LANES - SSM_HEADS)))
    wdt_hi = w_dt.astype(BF16)
    wdt_lo = (w_dt - wdt_hi.astype(F32)).astype(BF16)
    return dict(
        g_norm1=g_norm1.reshape(1, d), g_norm2=g_norm2.reshape(1, d),
        w_main=w_in[:, :main].astype(BF16), wdt_hi=wdt_hi, wdt_lo=wdt_lo,
        gq=(jnp.tile(g_q, SB_HEADS) * (SB_HEAD_DIM ** -0.5)).reshape(1, SB_WIDTH),
        gk=jnp.tile(g_k, SB_HEADS).reshape(1, SB_WIDTH),
        g_attn_out=g_attn_out.reshape(1, SB_WIDTH),
        conv_w=conv_w, conv_b=conv_b.reshape(1, CONV_DIM),
        dt_bias=_pad_lanes(dt_bias), a_log=_pad_lanes(a_log),
        d_skip=jnp.repeat(d_skip, SSM_HEAD_DIM).reshape(1, SSM_WIDTH),
        g_ssm_out=g_ssm_out.reshape(1, SSM_WIDTH),
        w_out=w_out.astype(BF16), w_gate=w_gate.astype(BF16), w_up=w_up.astype(BF16),
        w_down=w_down.astype(BF16))


def kernel(x_prompt, x_sample, cache_k, cache_v, state_conv, state_ssm, c_prompt, c_sample, w_ada, b_ada, g_norm1, w_in, g_q, g_k, g_attn_out, conv_w, conv_b, dt_bias, a_log, d_skip, g_ssm_out, w_out, g_norm2, w_gate, w_up, w_down):
    depth = w_in.shape[0]
    bp, d = c_prompt.shape
    bs = c_sample.shape[0]
    past = cache_k.shape[2]
    consts = _constants()
    xp, xs = x_prompt, x_sample
    outs = [[] for _ in range(8)]
    for i in range(depth):
        wts = _layer_weights(w_in[i], g_norm1[i], g_q[i], g_k[i], g_attn_out[i], conv_w[i], conv_b[i],
                             dt_bias[i], a_log[i], d_skip[i], g_ssm_out[i], w_out[i], g_norm2[i],
                             w_gate[i], w_up[i], w_down[i])
        mod = _modulation(jnp.concatenate([c_prompt, c_sample], axis=0), w_ada[i], b_ada[i])
        mod = mod.reshape(bp + bs, N_MOD, d)
        zero_conv = jnp.zeros((bp, CONV_WIDTH - 1, CONV_DIM), F32)
        zero_ssm = jnp.zeros((bp, SSM_HEADS, SSM_HEAD_DIM, D_STATE), F32)
        xp, kp, vp, cp, sp = _layer(xp, mod[:bp], None, None, zero_conv, zero_ssm, wts, consts)
        xs, ks, vs, cs, ss = _layer(
            xs, mod[bp:], cache_k[i].reshape(bs, past, SB_WIDTH), cache_v[i].reshape(bs, past, SB_WIDTH),
            state_conv[i], state_ssm[i], wts, consts)
        for lst, val in zip(outs, (kp, vp, cp, sp, ks, vs, cs, ss)):
            lst.append(val)
    return (xp, xs) + tuple(jnp.stack(o) for o in outs)
```

```python
import functools

import jax
import jax.numpy as jnp
from jax import lax
from jax.experimental import pallas as pl
from jax.experimental.pallas import tpu as pltpu

F32 = jnp.float32
BF16 = jnp.bfloat16

LANES = 128
SUBLANES = 8
VMEM_LIMIT_BYTES = 56 * 1024 * 1024

SB_HEADS = 8
SB_HEAD_DIM = 64
SB_WIDTH = SB_HEADS * SB_HEAD_DIM
SSM_HEADS = 8
SSM_HEAD_DIM = 64
SSM_WIDTH = SSM_HEADS * SSM_HEAD_DIM
SSM_GROUPS = 2
D_STATE = 128
CONV_WIDTH = 4
CONV_DIM = SSM_WIDTH + 2 * SSM_GROUPS * D_STATE
N_MOD = 6
EPS = 1e-6

KEY_STEP = 256
SSD_CHUNK = 128
HEADS_PER_SLAB = LANES // SB_HEAD_DIM


def _params(*sem):
    return pltpu.CompilerParams(dimension_semantics=sem, vmem_limit_bytes=VMEM_LIMIT_BYTES)


def _const_spec(shape):
    nd = len(shape)
    return pl.BlockSpec(shape, lambda *_: (0,) * nd, pipeline_mode=pl.Buffered(1))


def _split_bf16(v, terms):
    parts = []
    rem = v
    for _ in range(terms):
        p = rem.astype(BF16)
        parts.append(p)
        rem = rem - p.astype(F32)
    return parts


def _softplus(x):
    return jnp.maximum(x, 0.0) + jnp.log1p(jnp.exp(-jnp.abs(x)))


def _silu(x):
    return x / (1.0 + jnp.exp(-x))


def _mod_kernel(c_ref, w_ref, b_ref, o_ref):
    c = c_ref[...]
    a = _silu(c).astype(BF16)
    o_ref[...] = jnp.dot(a, w_ref[...].astype(BF16), preferred_element_type=F32) + b_ref[...]


def _modulation(c, w_ada, b_ada):
    n, d = c.shape
    cols = w_ada.shape[1]
    bn = d
    return pl.pallas_call(
        _mod_kernel,
        grid=(cols // bn,),
        in_specs=[pl.BlockSpec((n, d), lambda j: (0, 0)),
                  pl.BlockSpec((d, bn), lambda j: (0, j)),
                  pl.BlockSpec((1, bn), lambda j: (0, j))],
        out_specs=pl.BlockSpec((n, bn), lambda j: (0, j)),
        out_shape=jax.ShapeDtypeStruct((n, cols), F32),
        compiler_params=_params("parallel"),
        name="adaln_mod",
    )(c, w_ada, b_ada.reshape(1, cols))


def _inproj_kernel(x_ref, mod_ref, g1_ref, w_ref, gq_ref, gk_ref, bd_ref,
                   q_ref, k_ref, v_ref, kb_ref, vb_ref, z_ref, xbc_ref, dt_ref, *, tm, tq):
    x = x_ref[0]
    mod = mod_ref[0]
    shift1 = mod[0:1]
    scale1 = mod[1:2]
    ms = jnp.mean(x * x, axis=-1, keepdims=True)
    h = x * lax.rsqrt(ms + EPS) * g1_ref[...]
    h = h * (1.0 + scale1) + shift1
    hb = h.astype(BF16)

    def proj(lo, hi):
        return jnp.dot(hb, w_ref[:, lo:hi], preferred_element_type=F32)

    w = SB_WIDTH
    q = proj(0, w)
    k = proj(w, 2 * w)
    half = bd_ref.shape[0]
    sq = jnp.concatenate([(t * t).astype(BF16)[:, c:c + half] for t in (q, k) for c in range(0, w, half)],
                         axis=0)
    msq = jnp.dot(sq, bd_ref[...], preferred_element_type=F32)
    msq_q = jnp.concatenate([msq[i * tm:(i + 1) * tm] for i in range(0, w // half)], axis=1)
    msq_k = jnp.concatenate([msq[i * tm:(i + 1) * tm] for i in range(w // half, 2 * w // half)], axis=1)
    q = q * lax.rsqrt(msq_q + EPS) * gq_ref[...]
    k = k * lax.rsqrt(msq_k + EPS) * gk_ref[...]
    lane = lax.broadcasted_iota(jnp.int32, (1, LANES), 1)
    for tile in range(tm // tq):
        for slab in range(SB_WIDTH // LANES):
            q_slab = q[tile * tq:(tile + 1) * tq, slab * LANES:(slab + 1) * LANES]
            for e in range(HEADS_PER_SLAB):
                q_ref[0, tile, slab, e * tq:(e + 1) * tq, :] = jnp.where(
                    (lane // SB_HEAD_DIM) == e, q_slab, 0.0).astype(BF16)
    k_ref[0] = k
    kb_ref[0] = k.astype(BF16)
    v = proj(2 * w, 3 * w)
    v_ref[0] = v
    vb_ref[0] = v.astype(BF16)
    z_ref[0] = proj(3 * w, 3 * w + SSM_WIDTH).astype(BF16)
    main = 3 * w + SSM_WIDTH + CONV_DIM
    xbc_ref[0] = proj(3 * w + SSM_WIDTH, main)
    dt_ref[0] = proj(main, main + LANES)


def _in_projection(x, mod, g_norm1, w_in, gq, gk, bd, tm, tq):
    b, l, d = x.shape
    slabs = SB_WIDTH // LANES
    tok = lambda width: pl.BlockSpec((1, tm, width), lambda i, j: (i, j, 0))
    out_shapes = [
        jax.ShapeDtypeStruct((b, l // tq, slabs, HEADS_PER_SLAB * tq, LANES), BF16),
        jax.ShapeDtypeStruct((b, l, SB_WIDTH), F32),
        jax.ShapeDtypeStruct((b, l, SB_WIDTH), F32),
        jax.ShapeDtypeStruct((b, l, SB_WIDTH), BF16),
        jax.ShapeDtypeStruct((b, l, SB_WIDTH), BF16),
        jax.ShapeDtypeStruct((b, l, SSM_WIDTH), BF16),
        jax.ShapeDtypeStruct((b, l, CONV_DIM), F32),
        jax.ShapeDtypeStruct((b, l, LANES), F32),
    ]
    q_spec = pl.BlockSpec((1, tm // tq, slabs, HEADS_PER_SLAB * tq, LANES), lambda i, j: (i, j, 0, 0, 0))
    return pl.pallas_call(
        functools.partial(_inproj_kernel, tm=tm, tq=tq),
        grid=(b, l // tm),
        in_specs=[tok(d),
                  pl.BlockSpec((1, N_MOD, d), lambda i, j: (i, 0, 0)),
                  _const_spec((1, d)),
                  _const_spec(w_in.shape),
                  _const_spec((1, SB_WIDTH)),
                  _const_spec((1, SB_WIDTH)),
                  _const_spec(bd.shape)],
        out_specs=[q_spec] + [tok(s.shape[-1]) for s in out_shapes[1:]],
        out_shape=out_shapes,
        compiler_params=_params("parallel", "parallel"),
        name="in_projection",
    )(x, mod, g_norm1, w_in, gq, gk, bd)


def _attn_kernel(q_ref, k_ref, v_ref, tri_ref, g_ref, o_ref, acc_ref, sums_ref, *, tq, q_base):
    qi = pl.program_id(1)
    q0 = q_base + qi * tq
    step_diag = q0 // KEY_STEP
    tri = tri_ref[...]
    acc_ref[...] = jnp.zeros_like(acc_ref)
    sums_ref[...] = jnp.zeros_like(sums_ref)

    def key_step(ks, masked):
        start = pl.multiple_of(ks * KEY_STEP, KEY_STEP)
        if masked:
            visible = ((start + lax.broadcasted_iota(jnp.int32, (tq, KEY_STEP), 1))
                       < (q0 + lax.broadcasted_iota(jnp.int32, (tq, KEY_STEP), 0)))
            visible = jnp.concatenate([visible] * HEADS_PER_SLAB, axis=0)
        for slab in range(SB_WIDTH // LANES):
            cols = slice(slab * LANES, (slab + 1) * LANES)
            k_blk = k_ref[0, pl.ds(start, KEY_STEP), cols]
            v_blk = v_ref[0, pl.ds(start, KEY_STEP), cols]
            zs = lax.dot_general(q_ref[0, 0, slab], k_blk, (((1,), (1,)), ((), ())),
                                 preferred_element_type=F32)
            sp = jnp.maximum(zs, 0.0) + jnp.log(1.0 + jnp.exp(-jnp.abs(zs)))
            if masked:
                sp = jnp.where(visible, sp, 0.0)
            sums = sums_ref[slab]
            t = zs - sp - jnp.dot(sp.astype(BF16), tri, preferred_element_type=F32)
            w = jnp.concatenate(
                [jnp.exp(t[:, blk * LANES:(blk + 1) * LANES] - sums) for blk in range(KEY_STEP // LANES)],
                axis=1)
            sums_ref[slab] = sums + jnp.sum(sp, axis=1, keepdims=True)
            if masked:
                w = jnp.where(visible, w, 0.0)
            acc_ref[slab] += jnp.dot(w.astype(BF16), v_blk, preferred_element_type=F32)

    key_step(step_diag, True)

    def body(i, carry):
        key_step(step_diag - 1 - i, False)
        return carry

    lax.fori_loop(0, step_diag, body, 0)

    lane = lax.broadcasted_iota(jnp.int32, (1, LANES), 1)
    o = jnp.concatenate(
        [sum(jnp.where((lane // SB_HEAD_DIM) == e, acc_ref[slab, e * tq:(e + 1) * tq, :], 0.0)
             for e in range(HEADS_PER_SLAB))
         for slab in range(SB_WIDTH // LANES)], axis=1)
    ms = jnp.mean(o * o, axis=-1, keepdims=True)
    o_ref[0] = (o * lax.rsqrt(ms + EPS) * g_ref[...]).astype(o_ref.dtype)


def _attention(q, k, v, tri, g_attn, tq, q_base):
    b, nq, slabs, rows, _ = q.shape
    lk, w = k.shape[1:]
    lq = nq * tq
    return pl.pallas_call(
        functools.partial(_attn_kernel, tq=tq, q_base=q_base),
        grid=(b, nq),
        in_specs=[pl.BlockSpec((1, 1, slabs, rows, LANES), lambda i, j: (i, j, 0, 0, 0)),
                  pl.BlockSpec((1, lk, w), lambda i, j: (i, 0, 0)),
                  pl.BlockSpec((1, lk, w), lambda i, j: (i, 0, 0)),
                  _const_spec(tri.shape),
                  _const_spec((1, w))],
        out_specs=pl.BlockSpec((1, tq, w), lambda i, j: (i, j, 0)),
        out_shape=jax.ShapeDtypeStruct((b, lq, w), BF16),
        scratch_shapes=[pltpu.VMEM((slabs, rows, LANES), F32),
                        pltpu.VMEM((slabs, rows, LANES), F32)],
        compiler_params=_params("parallel", "parallel"),
        name="stick_breaking_attention",
    )(q, k, v, tri, g_attn)


def _ssd_kernel(xbc_ref, z_ref, dt_ref, convp_ref, ssmp_ref, cw_ref, cb_ref, dtb_ref, alog_ref,
                dsk_ref, g_ref, e_ref, tri_ref,
                y_ref, convo_ref, ssmo_ref, xpad_ref, st_ref, *, t, valid):
    c = pl.program_id(1)
    last = pl.num_programs(1) - 1
    tail = CONV_WIDTH - 1

    @pl.when(c == 0)
    def _():
        xpad_ref[0:SUBLANES, :] = jnp.zeros((SUBLANES, CONV_DIM), F32)
        xpad_ref[SUBLANES - tail:SUBLANES, :] = convp_ref[0]
        st_ref[...] = ssmp_ref[0].reshape(SSM_WIDTH, D_STATE).T

    xr = xbc_ref[0]
    xpad_ref[SUBLANES:SUBLANES + t, :] = xr
    conv = cb_ref[...] + cw_ref[CONV_WIDTH - 1:CONV_WIDTH, :] * xr
    for i in range(tail):
        conv = conv + cw_ref[i:i + 1, :] * xpad_ref[SUBLANES - tail + i:SUBLANES - tail + i + t, :]
    xpad_ref[0:SUBLANES, :] = xr[t - SUBLANES:t, :]
    convo_ref[0] = xr[valid - tail:valid, :]
    xbc = _silu(conv)
    xs = xbc[:, :SSM_WIDTH]
    bm = xbc[:, SSM_WIDTH:SSM_WIDTH + SSM_GROUPS * D_STATE]
    cm = xbc[:, SSM_WIDTH + SSM_GROUPS * D_STATE:]

    lane = lax.broadcasted_iota(jnp.int32, (1, LANES), 1)
    dt = _softplus(dt_ref[0] + dtb_ref[...])
    if valid < t:
        row = lax.broadcasted_iota(jnp.int32, (t, LANES), 0)
        dt = jnp.where(row < valid, dt, 0.0)
    a = jnp.where(lane < SSM_HEADS, -jnp.exp(alog_ref[...]), 0.0)
    tri = tri_ref[...]
    acum = sum(jnp.dot(tri, p, preferred_element_type=F32) for p in _split_bf16(dt * a, 3))
    acum_last = acum[t - 1:t, :]
    acum_t = acum.T
    dt_t = dt.T

    def expand(v):
        return sum(jnp.dot(p, e_ref[...], preferred_element_type=F32) for p in _split_bf16(v, 2))

    decay_in = expand(jnp.exp(acum))
    weight = expand(jnp.exp(acum_last - acum) * dt)
    chunk_decay = decay_in[t - 1:t, :]

    xs_b = xs.astype(BF16)
    xw = (xs * weight).astype(BF16)
    st = st_ref[...]
    st_b = st.astype(BF16)
    causal = (lax.broadcasted_iota(jnp.int32, (t, t), 0) >= lax.broadcasted_iota(jnp.int32, (t, t), 1))
    head_lanes = [(lane // SSM_HEAD_DIM) == e for e in range(HEADS_PER_SLAB)]
    group_w = SSM_WIDTH // SSM_GROUPS
    heads_per_group = SSM_HEADS // SSM_GROUPS

    y_diag = []
    y_off = []
    new_state = []
    for g in range(SSM_GROUPS):
        b_g = bm[:, g * D_STATE:(g + 1) * D_STATE]
        c_g = cm[:, g * D_STATE:(g + 1) * D_STATE].astype(BF16)
        b_gt = b_g.T.astype(BF16)
        gcols = slice(g * group_w, (g + 1) * group_w)
        new_state.append(jnp.dot(b_gt, xw[:, gcols], preferred_element_type=F32))
        y_off.append(jnp.dot(c_g, st_b[:, gcols], preferred_element_type=F32))
        cb = jnp.dot(c_g, b_gt, preferred_element_type=F32)
        for slab in range(heads_per_group // HEADS_PER_SLAB):
            mixes = []
            for e in range(HEADS_PER_SLAB):
                hd = g * heads_per_group + slab * HEADS_PER_SLAB + e
                seg = acum[:, hd:hd + 1] - acum_t[hd:hd + 1, :]
                decay = jnp.exp(jnp.where(causal, seg, -jnp.inf))
                mixes.append((cb * decay * dt_t[hd:hd + 1, :]).astype(BF16))
            s0 = (g * heads_per_group // HEADS_PER_SLAB + slab) * LANES
            x_slab = xs_b[:, s0:s0 + LANES]
            x_heads = [jnp.where(m, x_slab, jnp.zeros_like(x_slab)) for m in head_lanes]
            y_diag.append(jnp.dot(jnp.concatenate(mixes, axis=1), jnp.concatenate(x_heads, axis=0),
                                  preferred_element_type=F32))

    st_new = st * chunk_decay + jnp.concatenate(new_state, axis=1)
    st_ref[...] = st_new

    @pl.when(c == last)
    def _():
        ssmo_ref[0] = st_new.T.reshape(SSM_HEADS, SSM_HEAD_DIM, D_STATE)

    y = (jnp.concatenate(y_diag, axis=1) + jnp.concatenate(y_off, axis=1) * decay_in
         + dsk_ref[...] * xs)
    yg = y * _silu(z_ref[0].astype(F32))
    ms = jnp.mean(yg * yg, axis=-1, keepdims=True)
    y_ref[0] = (yg * lax.rsqrt(ms + EPS) * g_ref[...]).astype(y_ref.dtype)


def _ssd_mixer(xbc, z, dt, conv_past, ssm_past, conv_w, conv_b, dtb, alog, dsk, g_ssm, expand_mat, tri, valid):
    b, l, _ = xbc.shape
    t = SSD_CHUNK
    tail = CONV_WIDTH - 1
    assert valid >= tail and (valid == l or l == t)
    tok = lambda width: pl.BlockSpec((1, t, width), lambda i, j: (i, j, 0))
    out_shapes = [
        jax.ShapeDtypeStruct((b, l, SSM_WIDTH), BF16),
        jax.ShapeDtypeStruct((b, tail, CONV_DIM), F32),
        jax.ShapeDtypeStruct((b, SSM_HEADS, SSM_HEAD_DIM, D_STATE), F32),
    ]
    return pl.pallas_call(
        functools.partial(_ssd_kernel, t=t, valid=min(valid, t)),
        grid=(b, l // t),
        in_specs=[tok(CONV_DIM), tok(SSM_WIDTH), tok(LANES),
                  pl.BlockSpec((1, tail, CONV_DIM), lambda i, j: (i, 0, 0)),
                  pl.BlockSpec((1, SSM_HEADS, SSM_HEAD_DIM, D_STATE), lambda i, j: (i, 0, 0, 0)),
                  _const_spec((CONV_WIDTH, CONV_DIM)),
                  _const_spec((1, CONV_DIM)),
                  _const_spec((1, LANES)),
                  _const_spec((1, LANES)),
                  _const_spec((1, SSM_WIDTH)),
                  _const_spec((1, SSM_WIDTH)),
                  _const_spec(expand_mat.shape),
                  _const_spec(tri.shape)],
        out_specs=[tok(SSM_WIDTH),
                   pl.BlockSpec((1, tail, CONV_DIM), lambda i, j: (i, 0, 0)),
                   pl.BlockSpec((1, SSM_HEADS, SSM_HEAD_DIM, D_STATE), lambda i, j: (i, 0, 0, 0))],
        out_shape=out_shapes,
        scratch_shapes=[pltpu.VMEM((SUBLANES + t, CONV_DIM), F32),
                        pltpu.VMEM((D_STATE, SSM_WIDTH), F32)],
        compiler_params=_params("parallel", "arbitrary"),
        name="ssd_mixer",
    )(xbc, z, dt, conv_past, ssm_past, conv_w, conv_b, dtb, alog, dsk, g_ssm, expand_mat, tri)


def _ffn_chunks(d_ff):
    step = 4 * 256
    return [(s, min(s + step, d_ff)) for s in range(0, d_ff, step)]


def _outffn_kernel(x_ref, attn_ref, ssm_ref, mod_ref, g2_ref, wo_ref, wg_ref, wu_ref, wd_ref, y_ref):
    x = x_ref[0]
    mod = mod_ref[0]
    gate1, shift2, scale2, gate2 = mod[2:3], mod[3:4], mod[4:5], mod[5:6]
    mixed = (jnp.dot(attn_ref[0], wo_ref[:SB_WIDTH, :], preferred_element_type=F32)
             + jnp.dot(ssm_ref[0], wo_ref[SB_WIDTH:, :], preferred_element_type=F32))
    x1 = x + gate1 * mixed
    ms = jnp.mean(x1 * x1, axis=-1, keepdims=True)
    h2 = x1 * lax.rsqrt(ms + EPS) * g2_ref[...]
    h2 = (h2 * (1.0 + scale2) + shift2).astype(BF16)
    ff = jnp.zeros_like(x)
    for lo, hi in _ffn_chunks(wg_ref.shape[1]):
        gate = jnp.dot(h2, wg_ref[:, lo:hi], preferred_element_type=F32)
        up = jnp.dot(h2, wu_ref[:, lo:hi], preferred_element_type=F32)
        act = (_silu(gate) * up).astype(BF16)
        ff = ff + jnp.dot(act, wd_ref[lo:hi, :], preferred_element_type=F32)
    y_ref[0] = x1 + gate2 * ff


def _out_ffn(x, attn, ssm, mod, g_norm2, w_out, w_gate, w_up, w_down, tm):
    b, l, d = x.shape
    d_ff = w_gate.shape[1]
    tok = lambda width: pl.BlockSpec((1, tm, width), lambda i, j: (i, j, 0))
    return pl.pallas_call(
        _outffn_kernel,
        grid=(b, l // tm),
        in_specs=[tok(d), tok(SB_WIDTH), tok(SSM_WIDTH),
                  pl.BlockSpec((1, N_MOD, d), lambda i, j: (i, 0, 0)),
                  _const_spec((1, d)),
                  _const_spec(w_out.shape),
                  _const_spec((d, d_ff)),
                  _const_spec((d, d_ff)),
                  _const_spec((d_ff, d))],
        out_specs=tok(d),
        out_shape=jax.ShapeDtypeStruct((b, l, d), F32),
        compiler_params=_params("parallel", "parallel"),
        name="out_proj_ffn",
    )(x, attn, ssm, mod, g_norm2, w_out, w_gate, w_up, w_down)


def _token_tile(l):
    return min(l, 512)


def _pad_rows(a, rows):
    return jnp.pad(a, ((0, 0), (0, rows - a.shape[1]), (0, 0)))


def _layer(x, mod, k_past, v_past, conv_past, ssm_past, wts, consts):
    b, l, d = x.shape
    tm = _token_tile(l)
    tq = min(l, KEY_STEP)
    assert l % tq == 0 and tm % tq == 0
    q, k, v, kb, vb, z, xbc, dt = _in_projection(
        x, mod, wts["g_norm1"], wts["w_in"], wts["gq"], wts["gk"], consts["head_mean"], tm, tq)

    if k_past is None:
        attn = _attention(q, kb, vb, consts["attn_tri"], wts["g_attn_out"], tq, 0)
    else:
        past = k_past.shape[1]
        assert past % KEY_STEP == 0 and l <= KEY_STEP
        k_all = jnp.concatenate([k_past.astype(BF16), _pad_rows(kb, KEY_STEP)], axis=1)
        v_all = jnp.concatenate([v_past.astype(BF16), _pad_rows(vb, KEY_STEP)], axis=1)
        attn = _attention(q, k_all, v_all, consts["attn_tri"], wts["g_attn_out"], tq, past)

    if l % SSD_CHUNK:
        assert l < SSD_CHUNK
        xbc_p, z_p, dt_p = (_pad_rows(t, SSD_CHUNK) for t in (xbc, z, dt))
    else:
        xbc_p, z_p, dt_p = xbc, z, dt
    ssm, new_conv, new_ssm = _ssd_mixer(
        xbc_p, z_p, dt_p, conv_past, ssm_past, wts["conv_w"], wts["conv_b"], wts["dt_bias"],
        wts["a_log"], wts["d_skip"], wts["g_ssm_out"], consts["expand"], consts["scan_tri"], l)
    ssm = ssm[:, :l]

    y = _out_ffn(x, attn, ssm, mod, wts["g_norm2"], wts["w_out"], wts["w_gate"], wts["w_up"],
                 wts["w_down"], tm)
    hd = (b, l, SB_HEADS, SB_HEAD_DIM)
    return y, k.reshape(hd), v.reshape(hd), new_conv, new_ssm


def _constants():
    i = jnp.arange(KEY_STEP)
    attn_tri = (i[:, None] > i[None, :]).astype(BF16)
    r = jnp.arange(SSD_CHUNK)
    scan_tri = (r[None, :] <= r[:, None]).astype(BF16)
    heads = jnp.arange(LANES)[:, None]
    expand = (heads == (jnp.arange(SSM_WIDTH)[None, :] // SSM_HEAD_DIM)).astype(BF16)
    c = jnp.arange(2 * LANES) // SB_HEAD_DIM
    head_mean = ((c[:, None] == c[None, :]).astype(F32) / SB_HEAD_DIM).astype(BF16)
    return dict(attn_tri=attn_tri, scan_tri=scan_tri, expand=expand, head_mean=head_mean)


def _pad_lanes(v):
    return jnp.pad(v, (0, LANES - v.shape[0])).reshape(1, LANES)


def _layer_weights(w_in, g_norm1, g_q, g_k, g_attn_out, conv_w, conv_b, dt_bias, a_log, d_skip,
                   g_ssm_out, w_out, g_norm2, w_gate, w_up, w_down):
    d = w_in.shape[0]
    return dict(
        g_norm1=g_norm1.reshape(1, d), g_norm2=g_norm2.reshape(1, d),
        w_in=jnp.pad(w_in, ((0, 0), (0, LANES - SSM_HEADS))).astype(BF16),
        gq=(jnp.tile(g_q, SB_HEADS) * (SB_HEAD_DIM ** -0.5)).reshape(1, SB_WIDTH),
        gk=jnp.tile(g_k, SB_HEADS).reshape(1, SB_WIDTH),
        g_attn_out=g_attn_out.reshape(1, SB_WIDTH),
        conv_w=conv_w, conv_b=conv_b.reshape(1, CONV_DIM),
        dt_bias=_pad_lanes(dt_bias), a_log=_pad_lanes(a_log),
        d_skip=jnp.repeat(d_skip, SSM_HEAD_DIM).reshape(1, SSM_WIDTH),
        g_ssm_out=g_ssm_out.reshape(1, SSM_WIDTH),
        w_out=w_out.astype(BF16), w_gate=w_gate.astype(BF16), w_up=w_up.astype(BF16),
        w_down=w_down.astype(BF16))


def kernel(x_prompt, x_sample, cache_k, cache_v, state_conv, state_ssm, c_prompt, c_sample, w_ada, b_ada, g_norm1, w_in, g_q, g_k, g_attn_out, conv_w, conv_b, dt_bias, a_log, d_skip, g_ssm_out, w_out, g_norm2, w_gate, w_up, w_down):
    depth = w_in.shape[0]
    bp, d = c_prompt.shape
    bs = c_sample.shape[0]
    past = cache_k.shape[2]
    consts = _constants()
    xp, xs = x_prompt, x_sample
    outs = [[] for _ in range(8)]
    for i in range(depth):
        wts = _layer_weights(w_in[i], g_norm1[i], g_q[i], g_k[i], g_attn_out[i], conv_w[i], conv_b[i],
                             dt_bias[i], a_log[i], d_skip[i], g_ssm_out[i], w_out[i], g_norm2[i],
                             w_gate[i], w_up[i], w_down[i])
        mod = _modulation(jnp.concatenate([c_prompt, c_sample], axis=0), w_ada[i], b_ada[i])
        mod = mod.reshape(bp + bs, N_MOD, d)
        zero_conv = jnp.zeros((bp, CONV_WIDTH - 1, CONV_DIM), F32)
        zero_ssm = jnp.zeros((bp, SSM_HEADS, SSM_HEAD_DIM, D_STATE), F32)
        xp, kp, vp, cp, sp = _layer(xp, mod[:bp], None, None, zero_conv, zero_ssm, wts, consts)
        xs, ks, vs, cs, ss = _layer(
            xs, mod[bp:], cache_k[i].reshape(bs, past, SB_WIDTH), cache_v[i].reshape(bs, past, SB_WIDTH),
            state_conv[i], state_ssm[i], wts, consts)
        for lst, val in zip(outs, (kp, vp, cp, sp, ks, vs, cs, ss)):
            lst.append(val)
    return (xp, xs) + tuple(jnp.stack(o) for o in outs)
```
